```python
import math
import jax, jax.numpy as jnp
from jax import lax
import numpy as np

D_MODEL = 2048
BATCH = 8
SEQ = 2048
DEPTH = 2
DEC_BATCH = 1
DEC_SEQ = 8192
PAST_LEN = 128

HEAD_DIM = 64
N_HEADS_NA = 16
N_HEADS_WIN = 16
N_KV_WIN = 4
D_NA = N_HEADS_NA * HEAD_DIM
D_WIN = N_HEADS_WIN * HEAD_DIM
D_KV_WIN = N_KV_WIN * HEAD_DIM
D_MIX = D_NA + D_WIN
D_IN = 3 * D_NA + D_WIN + 2 * D_KV_WIN
SPLITS = (D_NA, 2 * D_NA, 3 * D_NA, 3 * D_NA + D_WIN, 3 * D_NA + D_WIN + D_KV_WIN)
D_FF = 5632
GRID_W = 64
NA_ROWS_MAX = 8
NA_COLS = 16
NA_QBLK_W = 16
NA_KSPAN_W = 32
WIN = 128
WIN_BLK = 128
DEEPNORM_ALPHA = float((2 * DEPTH) ** 0.25)
DEEPNORM_BETA = float((8 * DEPTH) ** -0.25)
LN_EPS = 1e-5
RMS_EPS = 1e-6
NEG_INF = -1e30
ATTN_SCALE = HEAD_DIM ** -0.5

kernel_name = "hybrid_na_wingqa_macaron_deepnorm_encoder"


def _alibi_slopes(n_heads):
    return np.asarray([2.0 ** (-8.0 * (h + 1) / n_heads) for h in range(n_heads)], dtype=np.float32)


def layer_norm(x, g, b):
    xf = x.astype(jnp.float32)
    mu = jnp.mean(xf, axis=-1, keepdims=True)
    var = jnp.mean(jnp.square(xf - mu), axis=-1, keepdims=True)
    y = (xf - mu) * lax.rsqrt(var + LN_EPS) * g.astype(jnp.float32) + b.astype(jnp.float32)
    return y.astype(x.dtype)


def rms_norm(x, g):
    xf = x.astype(jnp.float32)
    y = xf * lax.rsqrt(jnp.mean(jnp.square(xf), axis=-1, keepdims=True) + RMS_EPS) * g.astype(jnp.float32)
    return y.astype(x.dtype)


def swiglu(x, w_gate, w_up, w_down):
    return (jax.nn.silu(x @ w_gate) * (x @ w_up)) @ w_down


def neighbourhood_attention(q, k, v, rpb):
    B, T, H, d = q.shape
    R = T // GRID_W
    KH = min(NA_ROWS_MAX, R)
    NCB = GRID_W // NA_QBLK_W
    q = q.reshape(B, R, NCB, NA_QBLK_W, H, d)
    k = k.reshape(B, R, GRID_W, H, d)
    v = v.reshape(B, R, GRID_W, H, d)
    r = np.arange(R)
    row_start = np.clip(r - KH // 2, 0, R - KH)
    key_rows = row_start[:, None] + np.arange(KH)[None, :]
    cb = np.arange(NCB) * NA_QBLK_W
    span_start = np.clip(cb - NA_COLS // 2, 0, GRID_W - NA_KSPAN_W)
    key_cols = span_start[:, None] + np.arange(NA_KSPAN_W)[None, :]
    kg = k[:, key_rows[:, None, :, None], key_cols[None, :, None, :]]
    vg = v[:, key_rows[:, None, :, None], key_cols[None, :, None, :]]
    s = jnp.einsum('brcqhd,brckwhd->brchqkw', q, kg) * ATTN_SCALE
    q_cols = cb[:, None] + np.arange(NA_QBLK_W)[None, :]
    q_col_start = np.clip(q_cols - NA_COLS // 2, 0, GRID_W - NA_COLS)
    valid = (key_cols[:, None, :] >= q_col_start[:, :, None]) & (key_cols[:, None, :] < q_col_start[:, :, None] + NA_COLS)
    dr = key_rows - r[:, None] + (NA_ROWS_MAX - 1)
    dc = np.clip(key_cols[:, None, :] - q_cols[:, :, None], -(NA_COLS - 1), NA_COLS - 1) + (NA_COLS - 1)
    bias = rpb[:, dr[:, None, None, :, None], dc[None, :, :, None, :]]
    bias = jnp.moveaxis(bias, 0, 2)[None].astype(jnp.float32)
    logits = jnp.where(valid[None, None, :, None, :, None, :], s.astype(jnp.float32) + bias, NEG_INF)
    shp = logits.shape
    p = jax.nn.softmax(logits.reshape(shp[:-2] + (KH * NA_KSPAN_W,)), axis=-1).reshape(shp)
    o = jnp.einsum('brchqkw,brckwhd->brcqhd', p.astype(v.dtype), vg)
    return o.reshape(B, T, H * d)


def windowed_gqa(q, k, v, sink):
    B, T, H, d = q.shape
    KVH = k.shape[2]
    G = H // KVH
    nb = T // WIN_BLK
    q = q.reshape(B, nb, WIN_BLK, KVH, G, d)
    pad = ((0, 0), (WIN_BLK, WIN_BLK), (0, 0), (0, 0))
    kp = jnp.pad(k, pad).reshape(B, nb + 2, WIN_BLK, KVH, d)
    vp = jnp.pad(v, pad).reshape(B, nb + 2, WIN_BLK, KVH, d)
    kb = jnp.concatenate([kp[:, :-2], kp[:, 1:-1], kp[:, 2:]], axis=2)
    vb = jnp.concatenate([vp[:, :-2], vp[:, 1:-1], vp[:, 2:]], axis=2)
    s = jnp.einsum('bnqkgd,bnskd->bnkgqs', q, kb) * ATTN_SCALE
    t_pos = np.arange(nb)[:, None] * WIN_BLK + np.arange(WIN_BLK)[None, :]
    s_pos = np.arange(nb)[:, None] * WIN_BLK - WIN_BLK + np.arange(3 * WIN_BLK)[None, :]
    dist = np.abs(t_pos[:, :, None] - s_pos[:, None, :])
    valid = (dist <= WIN) & ((s_pos >= 0) & (s_pos < T))[:, None, :]
    slopes = _alibi_slopes(H).reshape(KVH, G)
    penalty = slopes[None, :, :, None, None] * dist.astype(np.float32)[:, None, None, :, :]
    logits = jnp.where(valid[None, :, None, None], s.astype(jnp.float32) - penalty[None], NEG_INF)
    sink_b = sink.astype(jnp.float32).reshape(KVH, G)[None, None, :, :, None]
    m = jnp.maximum(jnp.max(logits, axis=-1), sink_b)
    e = jnp.exp(logits - m[..., None])
    denom = jnp.sum(e, axis=-1) + jnp.exp(sink_b - m)
    p = e / denom[..., None]
    o = jnp.einsum('bnkgqs,bnskd->bnqkgd', p.astype(v.dtype), vb)
    return o.reshape(B, T, H * d)


def hybrid_mixer(x, w_in, w_out, rpb, sink, g_na, g_win):
    B, T, _ = x.shape
    h = x @ w_in
    qa, ka, va, qb, kb, vb = jnp.split(h, SPLITS, axis=-1)
    oa = neighbourhood_attention(qa.reshape(B, T, N_HEADS_NA, HEAD_DIM), ka.reshape(B, T, N_HEADS_NA, HEAD_DIM),
                                 va.reshape(B, T, N_HEADS_NA, HEAD_DIM), rpb)
    ob = windowed_gqa(qb.reshape(B, T, N_HEADS_WIN, HEAD_DIM), kb.reshape(B, T, N_KV_WIN, HEAD_DIM),
                      vb.reshape(B, T, N_KV_WIN, HEAD_DIM), sink)
    o = jnp.concatenate([rms_norm(oa, g_na), rms_norm(ob, g_win)], axis=-1)
    return o @ w_out


def encoder_trunk(x, w_in, w_out, na_rpb, win_sink, g_na, g_win,
                  ffn1_gate, ffn1_up, ffn1_down, ffn2_gate, ffn2_up, ffn2_down, ln_gain, ln_bias):
    a = DEEPNORM_ALPHA
    for l in range(DEPTH):
        x = layer_norm(a * x + 0.5 * swiglu(x, ffn1_gate[l], ffn1_up[l], ffn1_down[l]), ln_gain[l, 0], ln_bias[l, 0])
        x = layer_norm(a * x + hybrid_mixer(x, w_in[l], w_out[l], na_rpb[l], win_sink[l], g_na[l], g_win[l]),
                       ln_gain[l, 1], ln_bias[l, 1])
        x = layer_norm(a * x + 0.5 * swiglu(x, ffn2_gate[l], ffn2_up[l], ffn2_down[l]), ln_gain[l, 2], ln_bias[l, 2])
    return x


def _normal(k, shape, scale):
    return jax.random.normal(k, shape, jnp.float32) * scale


def setup_inputs(seed: int = 0) -> dict:
    key = jax.random.key(seed)
    ks = jax.random.split(key, 18)
    beta = DEEPNORM_BETA
    col_scale = np.ones((D_IN,), dtype=np.float32)
    col_scale[2 * D_NA:3 * D_NA] = beta
    col_scale[3 * D_NA + D_WIN + D_KV_WIN:] = beta
    w_in = _normal(ks[2], (DEPTH, D_MODEL, D_IN), D_MODEL ** -0.5) * jnp.asarray(col_scale)
    return {
        "x_prompt": _normal(ks[0], (BATCH, SEQ, D_MODEL), 1.0),
        "x_sample": _normal(ks[1], (DEC_BATCH, DEC_SEQ, D_MODEL), 1.0),
        "w_in": w_in,
        "w_out": _normal(ks[3], (DEPTH, D_MIX, D_MODEL), beta * D_MIX ** -0.5),
        "na_rpb": _normal(ks[4], (DEPTH, N_HEADS_NA, 2 * NA_ROWS_MAX - 1, 2 * NA_COLS - 1), 0.1),
        "win_sink": _normal(ks[5], (DEPTH, N_HEADS_WIN), 0.5),
        "g_na": 1.0 + _normal(ks[6], (DEPTH, D_NA), 0.02),
        "g_win": 1.0 + _normal(ks[7], (DEPTH, D_WIN), 0.02),
        "ffn1_gate": _normal(ks[8], (DEPTH, D_MODEL, D_FF), D_MODEL ** -0.5),
        "ffn1_up": _normal(ks[9], (DEPTH, D_MODEL, D_FF), beta * D_MODEL ** -0.5),
        "ffn1_down": _normal(ks[10], (DEPTH, D_FF, D_MODEL), beta * D_FF ** -0.5),
        "ffn2_gate": _normal(ks[11], (DEPTH, D_MODEL, D_FF), D_MODEL ** -0.5),
        "ffn2_up": _normal(ks[12], (DEPTH, D_MODEL, D_FF), beta * D_MODEL ** -0.5),
        "ffn2_down": _normal(ks[13], (DEPTH, D_FF, D_MODEL), beta * D_FF ** -0.5),
        "ln_gain": 1.0 + _normal(ks[14], (DEPTH, 3, D_MODEL), 0.02),
        "ln_bias": _normal(ks[15], (DEPTH, 3, D_MODEL), 0.02),
    }


def reference(x_prompt, x_sample, w_in, w_out, na_rpb, win_sink, g_na, g_win,
              ffn1_gate, ffn1_up, ffn1_down, ffn2_gate, ffn2_up, ffn2_down, ln_gain, ln_bias):
    y_prompt = encoder_trunk(x_prompt, w_in, w_out, na_rpb, win_sink, g_na, g_win,
                             ffn1_gate, ffn1_up, ffn1_down, ffn2_gate, ffn2_up, ffn2_down, ln_gain, ln_bias)
    y_sample = encoder_trunk(x_sample, w_in, w_out, na_rpb, win_sink, g_na, g_win,
                             ffn1_gate, ffn1_up, ffn1_down, ffn2_gate, ffn2_up, ffn2_down, ln_gain, ln_bias)
    return (y_prompt, y_sample)
```

```python
import functools

import numpy as np
import jax
import jax.numpy as jnp
from jax import lax
from jax.experimental import pallas as pl
from jax.experimental.pallas import tpu as pltpu

D_MODEL = 2048
DEPTH = 2
HEAD_DIM = 64
N_HEADS_NA = 16
N_HEADS_WIN = 16
N_KV_WIN = 4
GQA_GROUP = N_HEADS_WIN // N_KV_WIN
D_NA = N_HEADS_NA * HEAD_DIM
D_WIN = N_HEADS_WIN * HEAD_DIM
D_KV_WIN = N_KV_WIN * HEAD_DIM
D_IN = 3 * D_NA + D_WIN + 2 * D_KV_WIN
D_FF = 5632
GRID_W = 64
NA_ROWS = 8
NA_COLS = 16
WIN = 128
ALPHA = float((2 * DEPTH) ** 0.25)
LN_EPS = 1e-5
RMS_EPS = 1e-6
NEG_INF = -1e30
ATTN_SCALE = HEAD_DIM ** -0.5

BF16 = jnp.bfloat16
F32 = jnp.float32

Q_NA_COL, K_NA_COL, V_NA_COL, Q_WIN_COL = 0, 1, 2, 3
K_WIN_COL, V_WIN_COL = (3 * D_NA + D_WIN) // D_KV_WIN, (3 * D_NA + D_WIN) // D_KV_WIN + 1

FFN_TM = 512
FFN_TF = 512
PROJ_TM = 512
PROJ_TN = 512
NA_BLOCK_ROWS = 8
NA_BLOCK_TOK = NA_BLOCK_ROWS * GRID_W
NA_KEYS = NA_ROWS * GRID_W
WIN_QB = 512
VMEM_LIMIT = 56 * 1024 * 1024


def _layer_norm(y, gain, bias):
    mu = jnp.mean(y, axis=-1, keepdims=True)
    yc = y - mu
    var = jnp.mean(yc * yc, axis=-1, keepdims=True)
    return yc * lax.rsqrt(var + LN_EPS) * gain + bias


def _rms_norm(y, gain):
    return y * lax.rsqrt(jnp.mean(y * y, axis=-1, keepdims=True) + RMS_EPS) * gain


def _ffn_ln_kernel(x_ref, wg_ref, wu_ref, wd_ref, gain_ref, bias_ref, o_ref, xb_ref, acc_ref):
    j = pl.program_id(1)

    @pl.when(j == 0)
    def _():
        xb_ref[...] = x_ref[...].astype(BF16)
        acc_ref[...] = jnp.zeros_like(acc_ref)

    xb = xb_ref[...]
    g = jnp.dot(xb, wg_ref[...], preferred_element_type=F32)
    u = jnp.dot(xb, wu_ref[...], preferred_element_type=F32)
    h = (g * jax.nn.sigmoid(g) * u).astype(BF16)
    acc_ref[...] += jnp.dot(h, wd_ref[...], preferred_element_type=F32)

    @pl.when(j == pl.num_programs(1) - 1)
    def _():
        y = ALPHA * x_ref[...] + 0.5 * acc_ref[...]
        o_ref[...] = _layer_norm(y, gain_ref[...], bias_ref[...])


def _ffn_ln(x, wg, wu, wd, gain, bias, layer):
    t = x.shape[0]
    assert t % FFN_TM == 0 and D_FF % FFN_TF == 0
    return pl.pallas_call(
        _ffn_ln_kernel,
        grid=(t // FFN_TM, D_FF // FFN_TF),
        in_specs=[
            pl.BlockSpec((FFN_TM, D_MODEL), lambda i, j: (i, 0)),
            pl.BlockSpec((None, D_MODEL, FFN_TF), lambda i, j: (layer, 0, j)),
            pl.BlockSpec((None, D_MODEL, FFN_TF), lambda i, j: (layer, 0, j)),
            pl.BlockSpec((None, FFN_TF, D_MODEL), lambda i, j: (layer, j, 0)),
            pl.BlockSpec((1, D_MODEL), lambda i, j: (0, 0)),
            pl.BlockSpec((1, D_MODEL), lambda i, j: (0, 0)),
        ],
        out_specs=pl.BlockSpec((FFN_TM, D_MODEL), lambda i, j: (i, 0)),
        out_shape=jax.ShapeDtypeStruct((t, D_MODEL), F32),
        scratch_shapes=[pltpu.VMEM((FFN_TM, D_MODEL), BF16), pltpu.VMEM((FFN_TM, D_MODEL), F32)],
        compiler_params=pltpu.CompilerParams(
            dimension_semantics=("arbitrary", "arbitrary"), vmem_limit_bytes=VMEM_LIMIT),
        name="ffn_ln",
    )(x, wg, wu, wd, gain, bias)


def _in_proj_kernel(x_ref, w_ref, o_ref):
    xb = x_ref[...].astype(BF16)
    for n in range(D_IN // PROJ_TN):
        cols = slice(n * PROJ_TN, (n + 1) * PROJ_TN)
        o_ref[:, cols] = jnp.dot(xb, w_ref[:, cols], preferred_element_type=F32).astype(BF16)


def _in_proj(x, w_in, layer):
    t = x.shape[0]
    assert t % PROJ_TM == 0 and D_IN % PROJ_TN == 0
    return pl.pallas_call(
        _in_proj_kernel,
        grid=(t // PROJ_TM,),
        in_specs=[
            pl.BlockSpec((PROJ_TM, D_MODEL), lambda i: (i, 0)),
            pl.BlockSpec((None, D_MODEL, D_IN), lambda i: (layer, 0, 0), pipeline_mode=pl.Buffered(1)),
        ],
        out_specs=pl.BlockSpec((PROJ_TM, D_IN), lambda i: (i, 0)),
        out_shape=jax.ShapeDtypeStruct((t, D_IN), BF16),
        compiler_params=pltpu.CompilerParams(
            dimension_semantics=("arbitrary",), vmem_limit_bytes=VMEM_LIMIT),
        name="in_proj",
    )(x, w_in)


def _na_kernel(q_ref, kp_ref, kc_ref, kn_ref, vp_ref, vc_ref, vn_ref, bias_ref, mask_ref, gain_ref,
               o_ref, kwin_ref, vwin_ref, *, n_rows):
    r0 = pl.program_id(1) * NA_BLOCK_ROWS
    for part, (k_src, v_src) in enumerate(((kp_ref, vp_ref), (kc_ref, vc_ref), (kn_ref, vn_ref))):
        rows = slice(part * NA_BLOCK_TOK, (part + 1) * NA_BLOCK_TOK)
        kwin_ref[rows, :] = k_src[...]
        vwin_ref[rows, :] = v_src[...]
    valid = mask_ref[...] > 0.5
    gain = gain_ref[...]

    def row_body(i, carry):
        r = r0 + i
        row_start = jnp.clip(r - NA_ROWS // 2, 0, n_rows - NA_ROWS)
        off = pl.multiple_of((row_start - r0 + NA_BLOCK_ROWS) * GRID_W, GRID_W)
        dr0 = row_start - r + (NA_ROWS - 1)
        qrows = pl.ds(pl.multiple_of(i * GRID_W, GRID_W), GRID_W)
        q = q_ref[qrows, :] * ATTN_SCALE
        kw = kwin_ref[pl.ds(off, NA_KEYS), :]
        vw = vwin_ref[pl.ds(off, NA_KEYS), :]
        outs = []
        for h in range(N_HEADS_NA):
            hs = slice(h * HEAD_DIM, (h + 1) * HEAD_DIM)
            s = lax.dot_general(q[:, hs], kw[:, hs], (((1,), (1,)), ((), ())),
                                preferred_element_type=F32)
            bias = jnp.concatenate([bias_ref[h, dr0 + 2 * m] for m in range(NA_ROWS // 2)], axis=1)
            logits = jnp.where(valid, s + bias, NEG_INF)
            mx = jnp.max(logits, axis=-1, keepdims=True)
            p = jnp.exp(logits - mx)
            denom = jnp.sum(p, axis=-1, keepdims=True)
            o = jnp.dot(p.astype(BF16), vw[:, hs], preferred_element_type=F32)
            outs.append(o / denom)
        o_all = jnp.concatenate(outs, axis=1)
        o_ref[qrows, :] = _rms_norm(o_all, gain).astype(BF16)
        return carry

    lax.fori_loop(0, NA_BLOCK_ROWS, row_body, 0)


def _na_attention(h, bias2, mask, gain):
    b, t, _ = h.shape
    n_rows = t // GRID_W
    nblk = t // NA_BLOCK_TOK
    assert t % NA_BLOCK_TOK == 0 and n_rows >= NA_ROWS

    def blk(col, shift):
        return pl.BlockSpec((None, NA_BLOCK_TOK, D_NA),
                            lambda bi, i: (bi, jnp.clip(i + shift, 0, nblk - 1), col))

    return pl.pallas_call(
        functools.partial(_na_kernel, n_rows=n_rows),
        grid=(b, nblk),
        in_specs=[
            blk(Q_NA_COL, 0),
            blk(K_NA_COL, -1), blk(K_NA_COL, 0), blk(K_NA_COL, 1),
            blk(V_NA_COL, -1), blk(V_NA_COL, 0), blk(V_NA_COL, 1),
            pl.BlockSpec(bias2.shape, lambda bi, i: (0, 0, 0, 0), pipeline_mode=pl.Buffered(1)),
            pl.BlockSpec(mask.shape, lambda bi, i: (0, 0)),
            pl.BlockSpec((1, D_NA), lambda bi, i: (0, 0)),
        ],
        out_specs=pl.BlockSpec((None, NA_BLOCK_TOK, D_NA), lambda bi, i: (bi, i, 0)),
        out_shape=jax.ShapeDtypeStruct((b, t, D_NA), BF16),
        scratch_shapes=[pltpu.VMEM((3 * NA_BLOCK_TOK, D_NA), BF16),
                        pltpu.VMEM((3 * NA_BLOCK_TOK, D_NA), BF16)],
        compiler_params=pltpu.CompilerParams(
            dimension_semantics=("arbitrary", "arbitrary"), vmem_limit_bytes=VMEM_LIMIT),
        name="na_attn",
    )(h, h, h, h, h, h, h, bias2, mask, gain)


def _win_kernel(sink_ref, q_ref, kp_ref, kc_ref, kn_ref, vp_ref, vc_ref, vn_ref, gain_ref,
                o_ref, kwin_ref, vwin_ref, *, slopes):
    i = pl.program_id(1)
    nsub = WIN_QB // WIN
    kwin_ref[0:WIN, :] = kp_ref[...]
    kwin_ref[WIN:WIN + WIN_QB, :] = kc_ref[...]
    kwin_ref[WIN + WIN_QB:, :] = kn_ref[...]
    vwin_ref[0:WIN, :] = vp_ref[...]
    vwin_ref[WIN:WIN + WIN_QB, :] = vc_ref[...]
    vwin_ref[WIN + WIN_QB:, :] = vn_ref[...]
    gain = gain_ref[...]
    a = lax.broadcasted_iota(jnp.int32, (WIN, 3 * WIN), 0)
    c = lax.broadcasted_iota(jnp.int32, (WIN, 3 * WIN), 1)
    dist_i = jnp.abs(a - c + WIN)
    band = dist_i <= WIN
    dist = dist_i.astype(F32)
    first = i == 0
    last = i == pl.num_programs(1) - 1
    for sub in range(nsub):
        valid = band
        if sub == 0:
            valid = valid & (jnp.logical_not(first) | (c >= WIN))
        if sub == nsub - 1:
            valid = valid & (jnp.logical_not(last) | (c < 2 * WIN))
        qrows = slice(sub * WIN, (sub + 1) * WIN)
        q = q_ref[qrows, :] * ATTN_SCALE
        kw = kwin_ref[sub * WIN:(sub + 3) * WIN, :]
        vw = vwin_ref[sub * WIN:(sub + 3) * WIN, :]
        outs = []
        for hd in range(N_HEADS_WIN):
            kv = hd // GQA_GROUP
            hs = slice(hd * HEAD_DIM, (hd + 1) * HEAD_DIM)
            ks = slice(kv * HEAD_DIM, (kv + 1) * HEAD_DIM)
            s = lax.dot_general(q[:, hs], kw[:, ks], (((1,), (1,)), ((), ())),
                                preferred_element_type=F32)
            logits = jnp.where(valid, s - slopes[hd] * dist, NEG_INF)
            sink = sink_ref[hd]
            mx = jnp.maximum(jnp.max(logits, axis=-1, keepdims=True), sink)
            p = jnp.exp(logits - mx)
            denom = jnp.sum(p, axis=-1, keepdims=True) + jnp.exp(sink - mx)
            o = jnp.dot(p.astype(BF16), vw[:, ks], preferred_element_type=F32)
            outs.append(o / denom)
        o_all = jnp.concatenate(outs, axis=1)
        o_ref[qrows, :] = _rms_norm(o_all, gain).astype(BF16)


def _win_attention(h, sink, gain):
    b, t, _ = h.shape
    nblk = t // WIN_QB
    nsmall = t // WIN
    per = WIN_QB // WIN
    assert t % WIN_QB == 0
    slopes = tuple(float(np.float32(2.0 ** (-8.0 * (hd + 1) / N_HEADS_WIN))) for hd in range(N_HEADS_WIN))

    def cur(col):
        return pl.BlockSpec((None, WIN_QB, D_KV_WIN), lambda bi, i, s: (bi, i, col))

    def prev(col):
        return pl.BlockSpec((None, WIN, D_KV_WIN), lambda bi, i, s: (bi, jnp.maximum(i * per - 1, 0), col))

    def nxt(col):
        return pl.BlockSpec((None, WIN, D_KV_WIN),
                            lambda bi, i, s: (bi, jnp.minimum((i + 1) * per, nsmall - 1), col))

    grid_spec = pltpu.PrefetchScalarGridSpec(
        num_scalar_prefetch=1,
        grid=(b, nblk),
        in_specs=[
            pl.BlockSpec((None, WIN_QB, D_WIN), lambda bi, i, s: (bi, i, Q_WIN_COL)),
            prev(K_WIN_COL), cur(K_WIN_COL), nxt(K_WIN_COL),
            prev(V_WIN_COL), cur(V_WIN_COL), nxt(V_WIN_COL),
            pl.BlockSpec((1, D_WIN), lambda bi, i, s: (0, 0)),
        ],
        out_specs=pl.BlockSpec((None, WIN_QB, D_WIN), lambda bi, i, s: (bi, i, 0)),
        scratch_shapes=[pltpu.VMEM((WIN_QB + 2 * WIN, D_KV_WIN), BF16),
                        pltpu.VMEM((WIN_QB + 2 * WIN, D_KV_WIN), BF16)],
    )
    return pl.pallas_call(
        functools.partial(_win_kernel, slopes=slopes),
        grid_spec=grid_spec,
        out_shape=jax.ShapeDtypeStruct((b, t, D_WIN), BF16),
        compiler_params=pltpu.CompilerParams(
            dimension_semantics=("arbitrary", "arbitrary"), vmem_limit_bytes=VMEM_LIMIT),
        name="win_attn",
    )(sink, h, h, h, h, h, h, h, gain)


def _out_proj_kernel(x_ref, oa_ref, ob_ref, w_ref, gain_ref, bias_ref, o_ref):
    mix = jnp.dot(oa_ref[...], w_ref[0:D_NA, :], preferred_element_type=F32)
    mix = mix + jnp.dot(ob_ref[...], w_ref[D_NA:, :], preferred_element_type=F32)
    y = ALPHA * x_ref[...] + mix
    o_ref[...] = _layer_norm(y, gain_ref[...], bias_ref[...])


def _out_proj_ln(x, oa, ob, w_out, gain, bias, layer):
    t = x.shape[0]
    assert t % PROJ_TM == 0
    return pl.pallas_call(
        _out_proj_kernel,
        grid=(t // PROJ_TM,),
        in_specs=[
            pl.BlockSpec((PROJ_TM, D_MODEL), lambda i: (i, 0)),
            pl.BlockSpec((PROJ_TM, D_NA), lambda i: (i, 0)),
            pl.BlockSpec((PROJ_TM, D_WIN), lambda i: (i, 0)),
            pl.BlockSpec((None, D_NA + D_WIN, D_MODEL), lambda i: (layer, 0, 0), pipeline_mode=pl.Buffered(1)),
            pl.BlockSpec((1, D_MODEL), lambda i: (0, 0)),
            pl.BlockSpec((1, D_MODEL), lambda i: (0, 0)),
        ],
        out_specs=pl.BlockSpec((PROJ_TM, D_MODEL), lambda i: (i, 0)),
        out_shape=jax.ShapeDtypeStruct((t, D_MODEL), F32),
        compiler_params=pltpu.CompilerParams(
            dimension_semantics=("arbitrary",), vmem_limit_bytes=VMEM_LIMIT),
        name="out_proj_ln",
    )(x, oa, ob, w_out, gain, bias)


def _na_bias_table(rpb):
    qc = np.arange(GRID_W)[:, None]
    kc = np.arange(GRID_W)[None, :]
    dc = np.clip(kc - qc, -(NA_COLS - 1), NA_COLS - 1) + (NA_COLS - 1)
    e = rpb.astype(F32)[:, :, dc]
    return jnp.concatenate([e[:, :-1], e[:, 1:]], axis=-1)


def _na_valid_mask():
    qc = np.arange(GRID_W)[:, None]
    kc = np.arange(GRID_W)[None, :]
    start = np.clip(qc - NA_COLS // 2, 0, GRID_W - NA_COLS)
    valid = (kc >= start) & (kc < start + NA_COLS)
    return np.tile(valid, (1, NA_ROWS)).astype(np.float32)


def _trunk(x, params):
    b, t, _ = x.shape
    xf = x.reshape(b * t, D_MODEL)
    mask = jnp.asarray(_na_valid_mask())
    for l in range(DEPTH):
        ln_g = params["ln_gain"][l]
        ln_b = params["ln_bias"][l]
        xf = _ffn_ln(xf, params["ffn1_gate"], params["ffn1_up"], params["ffn1_down"],
                     ln_g[0:1], ln_b[0:1], l)
        h = _in_proj(xf, params["w_in"], l).reshape(b, t, D_IN)
        oa = _na_attention(h, params["na_bias"][l], mask, params["g_na"][l][None, :])
        ob = _win_attention(h, params["win_sink"][l], params["g_win"][l][None, :])
        xf = _out_proj_ln(xf, oa.reshape(b * t, D_NA), ob.reshape(b * t, D_WIN), params["w_out"],
                          ln_g[1:2], ln_b[1:2], l)
        xf = _ffn_ln(xf, params["ffn2_gate"], params["ffn2_up"], params["ffn2_down"],
                     ln_g[2:3], ln_b[2:3], l)
    return xf.reshape(b, t, D_MODEL)


def kernel(x_prompt, x_sample, w_in, w_out, na_rpb, win_sink, g_na, g_win, ffn1_gate, ffn1_up, ffn1_down,
           ffn2_gate, ffn2_up, ffn2_down, ln_gain, ln_bias):
    params = {
        "w_in": w_in.astype(BF16), "w_out": w_out.astype(BF16),
        "ffn1_gate": ffn1_gate.astype(BF16), "ffn1_up": ffn1_up.astype(BF16), "ffn1_down": ffn1_down.astype(BF16),
        "ffn2_gate": ffn2_gate.astype(BF16), "ffn2_up": ffn2_up.astype(BF16), "ffn2_down": ffn2_down.astype(BF16),
        "na_bias": jnp.stack([_na_bias_table(na_rpb[l]) for l in range(DEPTH)]),
        "win_sink": win_sink.astype(F32), "g_na": g_na.astype(F32), "g_win": g_win.astype(F32),
        "ln_gain": ln_gain.astype(F32), "ln_bias": ln_bias.astype(F32),
    }
    return (_trunk(x_prompt, params), _trunk(x_sample, params))
```

```python
import functools

import numpy as np
import jax
import jax.numpy as jnp
from jax import lax
from jax.experimental import pallas as pl
from jax.experimental.pallas import tpu as pltpu

D_MODEL = 2048
DEPTH = 2
HEAD_DIM = 64
N_HEADS_NA = 16
N_HEADS_WIN = 16
N_KV_WIN = 4
GQA_GROUP = N_HEADS_WIN // N_KV_WIN
D_NA = N_HEADS_NA * HEAD_DIM
D_WIN = N_HEADS_WIN * HEAD_DIM
D_KV_WIN = N_KV_WIN * HEAD_DIM
D_IN = 3 * D_NA + D_WIN + 2 * D_KV_WIN
D_FF = 5632
GRID_W = 64
NA_ROWS = 8
NA_COLS = 16
WIN = 128
ALPHA = float((2 * DEPTH) ** 0.25)
LN_EPS = 1e-5
RMS_EPS = 1e-6
NEG_INF = -1e30
ATTN_SCALE = HEAD_DIM ** -0.5

BF16 = jnp.bfloat16
F32 = jnp.float32

LANES = 128
N_SLABS_NA = D_NA // LANES

Q_NA_COL, K_NA_COL, V_NA_COL, Q_WIN_COL = 0, 1, 2, 3
K_WIN_COL, V_WIN_COL = (3 * D_NA + D_WIN) // D_KV_WIN, (3 * D_NA + D_WIN) // D_KV_WIN + 1

FFN_TM = 512
FFN_TF = 512
PROJ_TM = 512
PROJ_TN = 512
NA_BLOCK_ROWS = 8
NA_BLOCK_TOK = NA_BLOCK_ROWS * GRID_W
NA_WIN_TOK = 3 * NA_BLOCK_TOK
NA_KEYS = NA_ROWS * GRID_W
WIN_QB = 512
WIN_SUBS = WIN_QB // WIN
WIN_KEYS = 3 * WIN
WIN_RES_TOK = WIN_QB + 2 * WIN
VMEM_LIMIT = 56 * 1024 * 1024

_NT_DIMS = (((1,), (1,)), ((), ()))


def _layer_norm(y, gain, bias):
    mu = jnp.mean(y, axis=-1, keepdims=True)
    yc = y - mu
    var = jnp.mean(yc * yc, axis=-1, keepdims=True)
    return yc * lax.rsqrt(var + LN_EPS) * gain + bias


def _ffn_ln_kernel(x_ref, wg_ref, wu_ref, wd_ref, gain_ref, bias_ref, o_ref, xb_ref, acc_ref):
    j = pl.program_id(1)

    @pl.when(j == 0)
    def _():
        xb_ref[...] = x_ref[...].astype(BF16)
        acc_ref[...] = jnp.zeros_like(acc_ref)

    xb = xb_ref[...]
    g = jnp.dot(xb, wg_ref[...], preferred_element_type=F32)
    u = jnp.dot(xb, wu_ref[...], preferred_element_type=F32)
    h = (g * jax.nn.sigmoid(g) * u).astype(BF16)
    acc_ref[...] += jnp.dot(h, wd_ref[...], preferred_element_type=F32)

    @pl.when(j == pl.num_programs(1) - 1)
    def _():
        y = ALPHA * x_ref[...] + 0.5 * acc_ref[...]
        o_ref[...] = _layer_norm(y, gain_ref[...], bias_ref[...])


def _ffn_ln(x, wg, wu, wd, gain, bias, layer):
    t = x.shape[0]
    assert t % FFN_TM == 0 and D_FF % FFN_TF == 0
    return pl.pallas_call(
        _ffn_ln_kernel,
        grid=(t // FFN_TM, D_FF // FFN_TF),
        in_specs=[
            pl.BlockSpec((FFN_TM, D_MODEL), lambda i, j: (i, 0)),
            pl.BlockSpec((None, D_MODEL, FFN_TF), lambda i, j: (layer, 0, j)),
            pl.BlockSpec((None, D_MODEL, FFN_TF), lambda i, j: (layer, 0, j)),
            pl.BlockSpec((None, FFN_TF, D_MODEL), lambda i, j: (layer, j, 0)),
            pl.BlockSpec((1, D_MODEL), lambda i, j: (0, 0)),
            pl.BlockSpec((1, D_MODEL), lambda i, j: (0, 0)),
        ],
        out_specs=pl.BlockSpec((FFN_TM, D_MODEL), lambda i, j: (i, 0)),
        out_shape=jax.ShapeDtypeStruct((t, D_MODEL), F32),
        scratch_shapes=[pltpu.VMEM((FFN_TM, D_MODEL), BF16), pltpu.VMEM((FFN_TM, D_MODEL), F32)],
        compiler_params=pltpu.CompilerParams(
            dimension_semantics=("arbitrary", "arbitrary"), vmem_limit_bytes=VMEM_LIMIT),
        name="ffn_ln",
    )(x, wg, wu, wd, gain, bias)


def _in_proj_kernel(x_ref, w_ref, o_ref):
    xb = x_ref[...].astype(BF16)
    for n in range(D_IN // PROJ_TN):
        cols = slice(n * PROJ_TN, (n + 1) * PROJ_TN)
        o_ref[:, cols] = jnp.dot(xb, w_ref[:, cols], preferred_element_type=F32).astype(BF16)


def _in_proj(x, w_in, layer):
    t = x.shape[0]
    assert t % PROJ_TM == 0 and D_IN % PROJ_TN == 0
    return pl.pallas_call(
        _in_proj_kernel,
        grid=(t // PROJ_TM,),
        in_specs=[
            pl.BlockSpec((PROJ_TM, D_MODEL), lambda i: (i, 0)),
            pl.BlockSpec((None, D_MODEL, D_IN), lambda i: (layer, 0, 0), pipeline_mode=pl.Buffered(1)),
        ],
        out_specs=pl.BlockSpec((PROJ_TM, D_IN), lambda i: (i, 0)),
        out_shape=jax.ShapeDtypeStruct((t, D_IN), BF16),
        compiler_params=pltpu.CompilerParams(
            dimension_semantics=("arbitrary",), vmem_limit_bytes=VMEM_LIMIT),
        name="in_proj",
    )(x, w_in)


def _na_kernel(q_ref, k_ref, v_ref, bias_ref, mask_ref, gain_ref, o_ref, s_ref, oacc_ref, *, n_rows):
    r0 = pl.program_id(1) * NA_BLOCK_ROWS
    win_row0 = jnp.clip(r0 - NA_BLOCK_ROWS, 0, n_rows - 3 * NA_BLOCK_ROWS)
    valid = mask_ref[...] > 0.5
    gain = gain_ref[...]
    lo = lax.broadcasted_iota(jnp.int32, (GRID_W, LANES), 1) < HEAD_DIM

    def row_body(i, carry):
        r = r0 + i
        row_start = jnp.clip(r - NA_ROWS // 2, 0, n_rows - NA_ROWS)
        off = pl.multiple_of((row_start - win_row0) * GRID_W, GRID_W)
        dr0 = row_start - r + (NA_ROWS - 1)
        qrows = pl.ds(pl.multiple_of(i * GRID_W, GRID_W), GRID_W)
        krows = pl.ds(off, NA_KEYS)
        for p in range(N_SLABS_NA):
            cols = slice(p * LANES, (p + 1) * LANES)
            qs = q_ref[qrows, cols] * ATTN_SCALE
            zero = jnp.zeros_like(qs)
            lhs = jnp.concatenate([jnp.where(lo, qs, zero), jnp.where(lo, zero, qs)], axis=0)
            s = lax.dot_general(lhs, k_ref[krows, cols], _NT_DIMS, preferred_element_type=F32)
            bias = jnp.concatenate([bias_ref[p, dr0 + 2 * m] for m in range(NA_ROWS // 2)], axis=1)
            s_ref[p] = jnp.where(valid, s + bias, NEG_INF)
        ssq = jnp.zeros((GRID_W, 1), F32)
        for p in range(N_SLABS_NA):
            cols = slice(p * LANES, (p + 1) * LANES)
            x = s_ref[p]
            e = jnp.exp(x - jnp.max(x, axis=-1, keepdims=True))
            denom = jnp.sum(e, axis=-1, keepdims=True)
            res = jnp.dot(e.astype(BF16), v_ref[krows, cols], preferred_element_type=F32) * (1.0 / denom)
            o_pair = jnp.where(lo, res[0:GRID_W], res[GRID_W:])
            ssq = ssq + jnp.sum(o_pair * o_pair, axis=-1, keepdims=True)
            oacc_ref[:, cols] = o_pair
        inv = lax.rsqrt(ssq / D_NA + RMS_EPS)
        o_ref[qrows, :] = (oacc_ref[...] * inv * gain).astype(BF16)
        return carry

    lax.fori_loop(0, NA_BLOCK_ROWS, row_body, 0)


def _na_attention(h, bias2, mask, gain):
    b, t, _ = h.shape
    n_rows = t // GRID_W
    nblk = t // NA_BLOCK_TOK
    assert t % NA_BLOCK_TOK == 0 and t >= NA_WIN_TOK
    h2 = h.reshape(b * t, D_IN)

    def window(col):
        return pl.BlockSpec(
            (pl.Element(NA_WIN_TOK), pl.Element(D_NA)),
            lambda bi, i: (pl.multiple_of(bi * t + jnp.clip((i - 1) * NA_BLOCK_TOK, 0, t - NA_WIN_TOK),
                                          NA_BLOCK_TOK), col * D_NA))

    return pl.pallas_call(
        functools.partial(_na_kernel, n_rows=n_rows),
        grid=(b, nblk),
        in_specs=[
            pl.BlockSpec((None, NA_BLOCK_TOK, D_NA), lambda bi, i: (bi, i, Q_NA_COL)),
            window(K_NA_COL), window(V_NA_COL),
            pl.BlockSpec(bias2.shape, lambda bi, i: (0, 0, 0, 0), pipeline_mode=pl.Buffered(1)),
            pl.BlockSpec(mask.shape, lambda bi, i: (0, 0)),
            pl.BlockSpec((1, D_NA), lambda bi, i: (0, 0)),
        ],
        out_specs=pl.BlockSpec((None, NA_BLOCK_TOK, D_NA), lambda bi, i: (bi, i, 0)),
        out_shape=jax.ShapeDtypeStruct((b, t, D_NA), BF16),
        scratch_shapes=[pltpu.VMEM((N_SLABS_NA, 2 * GRID_W, NA_KEYS), F32),
                        pltpu.VMEM((GRID_W, D_NA), F32)],
        compiler_params=pltpu.CompilerParams(
            dimension_semantics=("arbitrary", "arbitrary"), vmem_limit_bytes=VMEM_LIMIT),
        name="na_attn",
    )(h, h2, h2, bias2, mask, gain)


def _win_kernel(sink_ref, q_ref, k_ref, v_ref, gain_ref, o_ref, s_ref, p_ref, oacc_ref, *, slopes, seq_len):
    i = pl.program_id(1)
    res_start = jnp.clip(i * WIN_QB - WIN, 0, seq_len - WIN_RES_TOK)
    gain = gain_ref[...]
    a_minus_c = (lax.broadcasted_iota(jnp.int32, (WIN, WIN_KEYS), 0)
                 - lax.broadcasted_iota(jnp.int32, (WIN, WIN_KEYS), 1))
    lo = lax.broadcasted_iota(jnp.int32, (WIN, LANES), 1) < HEAD_DIM

    def sub_body(sub, carry):
        q_start = i * WIN_QB + sub * WIN
        koff = pl.multiple_of(jnp.clip(q_start - WIN - res_start, 0, WIN_RES_TOK - WIN_KEYS), WIN)
        delta = q_start - res_start - koff
        dist_i = jnp.abs(a_minus_c + delta)
        band = dist_i <= WIN
        dist = dist_i.astype(F32)
        qrows = pl.ds(pl.multiple_of(sub * WIN, WIN), WIN)
        krows = pl.ds(koff, WIN_KEYS)
        for kv in range(N_KV_WIN):
            kcols = slice((kv // 2) * LANES, (kv // 2 + 1) * LANES)
            keep = lo if kv % 2 == 0 else jnp.logical_not(lo)
            pieces = []
            for g in range(GQA_GROUP):
                hd = kv * GQA_GROUP + g
                qs = q_ref[qrows, (hd // 2) * LANES:(hd // 2 + 1) * LANES] * ATTN_SCALE
                if hd % 2 != kv % 2:
                    qs = pltpu.roll(qs.astype(F32), HEAD_DIM, 1).astype(BF16)
                pieces.append(jnp.where(keep, qs, jnp.zeros_like(qs)))
            lhs = jnp.concatenate(pieces, axis=0)
            s = lax.dot_general(lhs, k_ref[krows, kcols], _NT_DIMS, preferred_element_type=F32)
            for g in range(GQA_GROUP):
                hd = kv * GQA_GROUP + g
                rows = slice(g * WIN, (g + 1) * WIN)
                s_ref[kv, rows, :] = jnp.where(band, s[rows] - slopes[hd] * dist, NEG_INF)
        ssq = jnp.zeros((WIN, 1), F32)
        for kv in range(N_KV_WIN):
            kcols = slice((kv // 2) * LANES, (kv // 2 + 1) * LANES)
            inv = []
            for g in range(GQA_GROUP):
                rows = slice(g * WIN, (g + 1) * WIN)
                x = s_ref[kv, rows, :]
                sink = sink_ref[kv * GQA_GROUP + g]
                mx = jnp.maximum(jnp.max(x, axis=-1, keepdims=True), sink)
                e = jnp.exp(x - mx)
                inv.append(1.0 / (jnp.sum(e, axis=-1, keepdims=True) + jnp.exp(sink - mx)))
                p_ref[kv, rows, :] = e.astype(BF16)
            res = jnp.dot(p_ref[kv], v_ref[krows, kcols], preferred_element_type=F32)
            for j in range(GQA_GROUP // 2):
                even = res[(2 * j) * WIN:(2 * j + 1) * WIN] * inv[2 * j]
                odd = res[(2 * j + 1) * WIN:(2 * j + 2) * WIN] * inv[2 * j + 1]
                if kv % 2 == 0:
                    o_pair = jnp.where(lo, even, pltpu.roll(odd, HEAD_DIM, 1))
                else:
                    o_pair = jnp.where(lo, pltpu.roll(even, HEAD_DIM, 1), odd)
                ssq = ssq + jnp.sum(o_pair * o_pair, axis=-1, keepdims=True)
                slab = kv * (GQA_GROUP // 2) + j
                oacc_ref[:, slab * LANES:(slab + 1) * LANES] = o_pair
        inv = lax.rsqrt(ssq / D_WIN + RMS_EPS)
        o_ref[qrows, :] = (oacc_ref[...] * inv * gain).astype(BF16)
        return carry

    lax.fori_loop(0, WIN_SUBS, sub_body, 0)


def _win_attention(h, sink, gain):
    b, t, _ = h.shape
    assert t % WIN_QB == 0 and t >= WIN_RES_TOK
    h2 = h.reshape(b * t, D_IN)
    slopes = tuple(float(np.float32(2.0 ** (-8.0 * (hd + 1) / N_HEADS_WIN))) for hd in range(N_HEADS_WIN))

    def window(col):
        return pl.BlockSpec(
            (pl.Element(WIN_RES_TOK), pl.Element(D_KV_WIN)),
            lambda bi, i, s: (pl.multiple_of(bi * t + jnp.clip(i * WIN_QB - WIN, 0, t - WIN_RES_TOK), WIN),
                              col * D_KV_WIN))

    grid_spec = pltpu.PrefetchScalarGridSpec(
        num_scalar_prefetch=1,
        grid=(b, t // WIN_QB),
        in_specs=[
            pl.BlockSpec((None, WIN_QB, D_WIN), lambda bi, i, s: (bi, i, Q_WIN_COL)),
            window(K_WIN_COL), window(V_WIN_COL),
            pl.BlockSpec((1, D_WIN), lambda bi, i, s: (0, 0)),
        ],
        out_specs=pl.BlockSpec((None, WIN_QB, D_WIN), lambda bi, i, s: (bi, i, 0)),
        scratch_shapes=[pltpu.VMEM((N_KV_WIN, GQA_GROUP * WIN, WIN_KEYS), F32),
                        pltpu.VMEM((N_KV_WIN, GQA_GROUP * WIN, WIN_KEYS), BF16),
                        pltpu.VMEM((WIN, D_WIN), F32)],
    )
    return pl.pallas_call(
        functools.partial(_win_kernel, slopes=slopes, seq_len=t),
        grid_spec=grid_spec,
        out_shape=jax.ShapeDtypeStruct((b, t, D_WIN), BF16),
        compiler_params=pltpu.CompilerParams(
            dimension_semantics=("arbitrary", "arbitrary"), vmem_limit_bytes=VMEM_LIMIT),
        name="win_attn",
    )(sink, h, h2, h2, gain)


def _out_proj_kernel(x_ref, oa_ref, ob_ref, w_ref, gain_ref, bias_ref, o_ref):
    mix = jnp.dot(oa_ref[...], w_ref[0:D_NA, :], preferred_element_type=F32)
    mix = mix + jnp.dot(ob_ref[...], w_ref[D_NA:, :], preferred_element_type=F32)
    y = ALPHA * x_ref[...] + mix
    o_ref[...] = _layer_norm(y, gain_ref[...], bias_ref[...])


def _out_proj_ln(x, oa, ob, w_out, gain, bias, layer):
    t = x.shape[0]
    assert t % PROJ_TM == 0
    return pl.pallas_call(
        _out_proj_kernel,
        grid=(t // PROJ_TM,),
        in_specs=[
            pl.BlockSpec((PROJ_TM, D_MODEL), lambda i: (i, 0)),
            pl.BlockSpec((PROJ_TM, D_NA), lambda i: (i, 0)),
            pl.BlockSpec((PROJ_TM, D_WIN), lambda i: (i, 0)),
            pl.BlockSpec((None, D_NA + D_WIN, D_MODEL), lambda i: (layer, 0, 0), pipeline_mode=pl.Buffered(1)),
            pl.BlockSpec((1, D_MODEL), lambda i: (0, 0)),
            pl.BlockSpec((1, D_MODEL), lambda i: (0, 0)),
        ],
        out_specs=pl.BlockSpec((PROJ_TM, D_MODEL), lambda i: (i, 0)),
        out_shape=jax.ShapeDtypeStruct((t, D_MODEL), F32),
        compiler_params=pltpu.CompilerParams(
            dimension_semantics=("arbitrary",), vmem_limit_bytes=VMEM_LIMIT),
        name="out_proj_ln",
    )(x, oa, ob, w_out, gain, bias)


def _na_bias_table(rpb):
    qc = np.arange(GRID_W)[:, None]
    kc = np.arange(GRID_W)[None, :]
    dc = np.clip(kc - qc, -(NA_COLS - 1), NA_COLS - 1) + (NA_COLS - 1)
    e = rpb.astype(F32)[:, :, dc]
    e2 = jnp.concatenate([e[:, :-1], e[:, 1:]], axis=-1)
    n_dr = e2.shape[1]
    e2 = e2.reshape(N_SLABS_NA, 2, n_dr, GRID_W, LANES).transpose(0, 2, 1, 3, 4)
    return e2.reshape(N_SLABS_NA, n_dr, 2 * GRID_W, LANES)


def _na_valid_mask():
    qc = np.arange(GRID_W)[:, None]
    kc = np.arange(GRID_W)[None, :]
    start = np.clip(qc - NA_COLS // 2, 0, GRID_W - NA_COLS)
    valid = (kc >= start) & (kc < start + NA_COLS)
    return np.tile(valid, (2, NA_ROWS)).astype(np.float32)


def _trunk(x, params):
    b, t, _ = x.shape
    xf = x.reshape(b * t, D_MODEL)
    mask = jnp.asarray(_na_valid_mask())
    for l in range(DEPTH):
        ln_g = params["ln_gain"][l]
        ln_b = params["ln_bias"][l]
        xf = _ffn_ln(xf, params["ffn1_gate"], params["ffn1_up"], params["ffn1_down"],
                     ln_g[0:1], ln_b[0:1], l)
        h = _in_proj(xf, params["w_in"], l).reshape(b, t, D_IN)
        oa = _na_attention(h, params["na_bias"][l], mask, params["g_na"][l][None, :])
        ob = _win_attention(h, params["win_sink"][l], params["g_win"][l][None, :])
        xf = _out_proj_ln(xf, oa.reshape(b * t, D_NA), ob.reshape(b * t, D_WIN), params["w_out"],
                          ln_g[1:2], ln_b[1:2], l)
        xf = _ffn_ln(xf, params["ffn2_gate"], params["ffn2_up"], params["ffn2_down"],
                     ln_g[2:3], ln_b[2:3], l)
    return xf.reshape(b, t, D_MODEL)


def kernel(x_prompt, x_sample, w_in, w_out, na_rpb, win_sink, g_na, g_win, ffn1_gate, ffn1_up, ffn1_down,
           ffn2_gate, ffn2_up, ffn2_down, ln_gain, ln_bias):
    params = {
        "w_in": w_in.astype(BF16), "w_out": w_out.astype(BF16),
        "ffn1_gate": ffn1_gate.astype(BF16), "ffn1_up": ffn1_up.astype(BF16), "ffn1_down": ffn1_down.astype(BF16),
        "ffn2_gate": ffn2_gate.astype(BF16), "ffn2_up": ffn2_up.astype(BF16), "ffn2_down": ffn2_down.astype(BF16),
        "na_bias": jnp.stack([_na_bias_table(na_rpb[l]) for l in range(DEPTH)]),
        "win_sink": win_sink.astype(F32), "g_na": g_na.astype(F32), "g_win": g_win.astype(F32),
        "ln_gain": ln_gain.astype(F32), "ln_bias": ln_bias.astype(F32),
    }
    return (_trunk(x_prompt, params), _trunk(x_sample, params))
```

```python
import functools

import numpy as np
import jax
import jax.numpy as jnp
from jax import lax
from jax.experimental import pallas as pl
from jax.experimental.pallas import tpu as pltpu

D_MODEL = 2048
DEPTH = 2
HEAD_DIM = 64
N_HEADS_NA = 16
N_HEADS_WIN = 16
N_KV_WIN = 4
GQA_GROUP = N_HEADS_WIN // N_KV_WIN
D_NA = N_HEADS_NA * HEAD_DIM
D_WIN = N_HEADS_WIN * HEAD_DIM
D_KV_WIN = N_KV_WIN * HEAD_DIM
D_IN = 3 * D_NA + D_WIN + 2 * D_KV_WIN
D_FF = 5632
GRID_W = 64
NA_ROWS = 8
NA_COLS = 16
WIN = 128
ALPHA = float((2 * DEPTH) ** 0.25)
LN_EPS = 1e-5
RMS_EPS = 1e-6
NEG_INF = -1e30
ATTN_SCALE = HEAD_DIM ** -0.5

BF16 = jnp.bfloat16
F32 = jnp.float32

LANES = 128
N_SLABS_NA = D_NA // LANES

Q_NA_COL, K_NA_COL, V_NA_COL, Q_WIN_COL = 0, 1, 2, 3
K_WIN_COL, V_WIN_COL = (3 * D_NA + D_WIN) // D_KV_WIN, (3 * D_NA + D_WIN) // D_KV_WIN + 1

FFN_TM = 1024
FFN_SUB = 512
FFN_TF = 512
PROJ_TM = 512
PROJ_TN = 512
OUT_TM = 1024
NA_BLOCK_ROWS = 8
NA_BLOCK_TOK = NA_BLOCK_ROWS * GRID_W
NA_WIN_TOK = 3 * NA_BLOCK_TOK
NA_KEYS = NA_ROWS * GRID_W
WIN_QB = 512
WIN_SUBS = WIN_QB // WIN
WIN_KEYS = 3 * WIN
WIN_RES_TOK = WIN_QB + 2 * WIN
VMEM_LIMIT = 56 * 1024 * 1024
FFN_VMEM_LIMIT = 62 * 1024 * 1024

_NT_DIMS = (((1,), (1,)), ((), ()))


def _layer_norm(y, gain, bias):
    mu = jnp.mean(y, axis=-1, keepdims=True)
    yc = y - mu
    var = jnp.mean(yc * yc, axis=-1, keepdims=True)
    return yc * lax.rsqrt(var + LN_EPS) * gain + bias


def _ffn_ln_kernel(x_ref, wg_ref, wu_ref, wd_ref, gain_ref, bias_ref, o_ref, xb_ref):
    j = pl.program_id(1)
    last_j = pl.num_programs(1) - 1

    def step(first, last):
        for r in range(FFN_TM // FFN_SUB):
            rows = slice(r * FFN_SUB, (r + 1) * FFN_SUB)
            if first:
                xb_ref[rows, :] = x_ref[rows, :].astype(BF16)
            xb = xb_ref[rows, :]
            g = jnp.dot(xb, wg_ref[...], preferred_element_type=F32)
            u = jnp.dot(xb, wu_ref[...], preferred_element_type=F32)
            h = (g * jax.nn.sigmoid(g) * u).astype(BF16)
            acc = jnp.dot(h, wd_ref[...], preferred_element_type=F32)
            if not first:
                acc = o_ref[rows, :] + acc
            if last:
                acc = _layer_norm(ALPHA * x_ref[rows, :] + 0.5 * acc, gain_ref[...], bias_ref[...])
            o_ref[rows, :] = acc

    pl.when(j == 0)(lambda: step(True, False))
    pl.when((j > 0) & (j < last_j))(lambda: step(False, False))
    pl.when(j == last_j)(lambda: step(False, True))


def _ffn_ln(x, wg, wu, wd, gain, bias, layer):
    t = x.shape[0]
    assert t % FFN_TM == 0 and D_FF % FFN_TF == 0
    return pl.pallas_call(
        _ffn_ln_kernel,
        grid=(t // FFN_TM, D_FF // FFN_TF),
        in_specs=[
            pl.BlockSpec((FFN_TM, D_MODEL), lambda i, j: (i, 0)),
            pl.BlockSpec((None, D_MODEL, FFN_TF), lambda i, j: (layer, 0, j)),
            pl.BlockSpec((None, D_MODEL, FFN_TF), lambda i, j: (layer, 0, j)),
            pl.BlockSpec((None, FFN_TF, D_MODEL), lambda i, j: (layer, j, 0)),
            pl.BlockSpec((1, D_MODEL), lambda i, j: (0, 0)),
            pl.BlockSpec((1, D_MODEL), lambda i, j: (0, 0)),
        ],
        out_specs=pl.BlockSpec((FFN_TM, D_MODEL), lambda i, j: (i, 0)),
        out_shape=jax.ShapeDtypeStruct((t, D_MODEL), F32),
        scratch_shapes=[pltpu.VMEM((FFN_TM, D_MODEL), BF16)],
        compiler_params=pltpu.CompilerParams(
            dimension_semantics=("arbitrary", "arbitrary"), vmem_limit_bytes=FFN_VMEM_LIMIT),
        name="ffn_ln",
    )(x, wg, wu, wd, gain, bias)


def _in_proj_kernel(x_ref, w_ref, o_ref):
    xb = x_ref[...].astype(BF16)
    for n in range(D_IN // PROJ_TN):
        cols = slice(n * PROJ_TN, (n + 1) * PROJ_TN)
        o_ref[:, cols] = jnp.dot(xb, w_ref[:, cols], preferred_element_type=F32).astype(BF16)


def _in_proj(x, w_in, layer):
    t = x.shape[0]
    assert t % PROJ_TM == 0 and D_IN % PROJ_TN == 0
    return pl.pallas_call(
        _in_proj_kernel,
        grid=(t // PROJ_TM,),
        in_specs=[
            pl.BlockSpec((PROJ_TM, D_MODEL), lambda i: (i, 0)),
            pl.BlockSpec((None, D_MODEL, D_IN), lambda i: (layer, 0, 0), pipeline_mode=pl.Buffered(1)),
        ],
        out_specs=pl.BlockSpec((PROJ_TM, D_IN), lambda i: (i, 0)),
        out_shape=jax.ShapeDtypeStruct((t, D_IN), BF16),
        compiler_params=pltpu.CompilerParams(
            dimension_semantics=("arbitrary",), vmem_limit_bytes=VMEM_LIMIT),
        name="in_proj",
    )(x, w_in)


def _na_kernel(q_ref, k_ref, v_ref, bias_ref, mask_ref, gain_ref, o_ref, s_ref, oacc_ref, *, n_rows):
    r0 = pl.program_id(1) * NA_BLOCK_ROWS
    win_row0 = jnp.clip(r0 - NA_BLOCK_ROWS, 0, n_rows - 3 * NA_BLOCK_ROWS)
    valid = mask_ref[...] > 0.5
    gain = gain_ref[...]
    lo = lax.broadcasted_iota(jnp.int32, (GRID_W, LANES), 1) < HEAD_DIM

    def row_geometry(i):
        r = r0 + i
        row_start = jnp.clip(r - NA_ROWS // 2, 0, n_rows - NA_ROWS)
        off = pl.multiple_of((row_start - win_row0) * GRID_W, GRID_W)
        dr0 = row_start - r + (NA_ROWS - 1)
        qrows = pl.ds(pl.multiple_of(i * GRID_W, GRID_W), GRID_W)
        return qrows, pl.ds(off, NA_KEYS), dr0

    def scores(geom, p):
        qrows, krows, dr0 = geom
        cols = slice(p * LANES, (p + 1) * LANES)
        qs = q_ref[qrows, cols] * ATTN_SCALE
        zero = jnp.zeros_like(qs)
        lhs = jnp.concatenate([jnp.where(lo, qs, zero), jnp.where(lo, zero, qs)], axis=0)
        s = lax.dot_general(lhs, k_ref[krows, cols], _NT_DIMS, preferred_element_type=F32)
        bias = jnp.concatenate([bias_ref[p, dr0 + 2 * m] for m in range(NA_ROWS // 2)], axis=1)
        s_ref[p] = jnp.where(valid, s + bias, NEG_INF)

    def attend(geom, p, ssq):
        _, krows, _ = geom
        cols = slice(p * LANES, (p + 1) * LANES)
        x = s_ref[p]
        e = jnp.exp(x - jnp.max(x, axis=-1, keepdims=True))
        denom = jnp.sum(e, axis=-1, keepdims=True)
        res = jnp.dot(e.astype(BF16), v_ref[krows, cols], preferred_element_type=F32) * (1.0 / denom)
        o_pair = jnp.where(lo, res[0:GRID_W], res[GRID_W:])
        oacc_ref[:, cols] = o_pair
        return ssq + jnp.sum(o_pair * o_pair, axis=-1, keepdims=True)

    def row_body(i, carry):
        geom = row_geometry(i)
        for p in range(N_SLABS_NA):
            scores(geom, p)
        ssq = jnp.zeros((GRID_W, 1), F32)
        for p in range(N_SLABS_NA):
            ssq = attend(geom, p, ssq)
        inv = lax.rsqrt(ssq / D_NA + RMS_EPS)
        o_ref[geom[0], :] = (oacc_ref[...] * inv * gain).astype(BF16)
        return carry

    lax.fori_loop(0, NA_BLOCK_ROWS, row_body, 0)


def _na_attention(h, bias2, mask, gain):
    b, t, _ = h.shape
    n_rows = t // GRID_W
    nblk = t // NA_BLOCK_TOK
    assert t % NA_BLOCK_TOK == 0 and t >= NA_WIN_TOK
    h2 = h.reshape(b * t, D_IN)

    def window(col):
        return pl.BlockSpec(
            (pl.Element(NA_WIN_TOK), pl.Element(D_NA)),
            lambda bi, i: (pl.multiple_of(bi * t + jnp.clip((i - 1) * NA_BLOCK_TOK, 0, t - NA_WIN_TOK),
                                          NA_BLOCK_TOK), col * D_NA))

    return pl.pallas_call(
        functools.partial(_na_kernel, n_rows=n_rows),
        grid=(b, nblk),
        in_specs=[
            pl.BlockSpec((None, NA_BLOCK_TOK, D_NA), lambda bi, i: (bi, i, Q_NA_COL)),
            window(K_NA_COL), window(V_NA_COL),
            pl.BlockSpec(bias2.shape, lambda bi, i: (0, 0, 0, 0), pipeline_mode=pl.Buffered(1)),
            pl.BlockSpec(mask.shape, lambda bi, i: (0, 0)),
            pl.BlockSpec((1, D_NA), lambda bi, i: (0, 0)),
        ],
        out_specs=pl.BlockSpec((None, NA_BLOCK_TOK, D_NA), lambda bi, i: (bi, i, 0)),
        out_shape=jax.ShapeDtypeStruct((b, t, D_NA), BF16),
        scratch_shapes=[pltpu.VMEM((N_SLABS_NA, 2 * GRID_W, NA_KEYS), F32),
                        pltpu.VMEM((GRID_W, D_NA), F32)],
        compiler_params=pltpu.CompilerParams(
            dimension_semantics=("arbitrary", "arbitrary"), vmem_limit_bytes=VMEM_LIMIT),
        name="na_attn",
    )(h, h2, h2, bias2, mask, gain)


def _win_kernel(sink_ref, q_ref, k_ref, v_ref, gain_ref, o_ref, s_ref, p_ref, oacc_ref, *, slopes, seq_len):
    i = pl.program_id(1)
    res_start = jnp.clip(i * WIN_QB - WIN, 0, seq_len - WIN_RES_TOK)
    gain = gain_ref[...]
    a_minus_c = (lax.broadcasted_iota(jnp.int32, (WIN, WIN_KEYS), 0)
                 - lax.broadcasted_iota(jnp.int32, (WIN, WIN_KEYS), 1))
    lo = lax.broadcasted_iota(jnp.int32, (WIN, LANES), 1) < HEAD_DIM

    def sub_body(sub, carry):
        q_start = i * WIN_QB + sub * WIN
        koff = pl.multiple_of(jnp.clip(q_start - WIN - res_start, 0, WIN_RES_TOK - WIN_KEYS), WIN)
        delta = q_start - res_start - koff
        dist_i = jnp.abs(a_minus_c + delta)
        band = dist_i <= WIN
        dist = dist_i.astype(F32)
        qrows = pl.ds(pl.multiple_of(sub * WIN, WIN), WIN)
        krows = pl.ds(koff, WIN_KEYS)
        for kv in range(N_KV_WIN):
            kcols = slice((kv // 2) * LANES, (kv // 2 + 1) * LANES)
            keep = lo if kv % 2 == 0 else jnp.logical_not(lo)
            pieces = []
            for g in range(GQA_GROUP):
                hd = kv * GQA_GROUP + g
                qs = q_ref[qrows, (hd // 2) * LANES:(hd // 2 + 1) * LANES] * ATTN_SCALE
                if hd % 2 != kv % 2:
                    qs = pltpu.roll(qs.astype(F32), HEAD_DIM, 1).astype(BF16)
                pieces.append(jnp.where(keep, qs, jnp.zeros_like(qs)))
            lhs = jnp.concatenate(pieces, axis=0)
            s = lax.dot_general(lhs, k_ref[krows, kcols], _NT_DIMS, preferred_element_type=F32)
            for g in range(GQA_GROUP):
                hd = kv * GQA_GROUP + g
                rows = slice(g * WIN, (g + 1) * WIN)
                s_ref[kv, rows, :] = jnp.where(band, s[rows] - slopes[hd] * dist, NEG_INF)
        ssq = jnp.zeros((WIN, 1), F32)
        for kv in range(N_KV_WIN):
            kcols = slice((kv // 2) * LANES, (kv // 2 + 1) * LANES)
            inv = []
            for g in range(GQA_GROUP):
                rows = slice(g * WIN, (g + 1) * WIN)
                x = s_ref[kv, rows, :]
                sink = sink_ref[kv * GQA_GROUP + g]
                mx = jnp.maximum(jnp.max(x, axis=-1, keepdims=True), sink)
                e = jnp.exp(x - mx)
                inv.append(1.0 / (jnp.sum(e, axis=-1, keepdims=True) + jnp.exp(sink - mx)))
                p_ref[kv, rows, :] = e.astype(BF16)
            res = jnp.dot(p_ref[kv], v_ref[krows, kcols], preferred_element_type=F32)
            for j in range(GQA_GROUP // 2):
                even = res[(2 * j) * WIN:(2 * j + 1) * WIN] * inv[2 * j]
                odd = res[(2 * j + 1) * WIN:(2 * j + 2) * WIN] * inv[2 * j + 1]
                if kv % 2 == 0:
                    o_pair = jnp.where(lo, even, pltpu.roll(odd, HEAD_DIM, 1))
                else:
                    o_pair = jnp.where(lo, pltpu.roll(even, HEAD_DIM, 1), odd)
                ssq = ssq + jnp.sum(o_pair * o_pair, axis=-1, keepdims=True)
                slab = kv * (GQA_GROUP // 2) + j
                oacc_ref[:, slab * LANES:(slab + 1) * LANES] = o_pair
        inv = lax.rsqrt(ssq / D_WIN + RMS_EPS)
        o_ref[qrows, :] = (oacc_ref[...] * inv * gain).astype(BF16)
        return carry

    lax.fori_loop(0, WIN_SUBS, sub_body, 0)


def _win_attention(h, sink, gain):
    b, t, _ = h.shape
    assert t % WIN_QB == 0 and t >= WIN_RES_TOK
    h2 = h.reshape(b * t, D_IN)
    slopes = tuple(float(np.float32(2.0 ** (-8.0 * (hd + 1) / N_HEADS_WIN))) for hd in range(N_HEADS_WIN))

    def window(col):
        return pl.BlockSpec(
            (pl.Element(WIN_RES_TOK), pl.Element(D_KV_WIN)),
            lambda bi, i, s: (pl.multiple_of(bi * t + jnp.clip(i * WIN_QB - WIN, 0, t - WIN_RES_TOK), WIN),
                              col * D_KV_WIN))

    grid_spec = pltpu.PrefetchScalarGridSpec(
        num_scalar_prefetch=1,
        grid=(b, t // WIN_QB),
        in_specs=[
            pl.BlockSpec((None, WIN_QB, D_WIN), lambda bi, i, s: (bi, i, Q_WIN_COL)),
            window(K_WIN_COL), window(V_WIN_COL),
            pl.BlockSpec((1, D_WIN), lambda bi, i, s: (0, 0)),
        ],
        out_specs=pl.BlockSpec((None, WIN_QB, D_WIN), lambda bi, i, s: (bi, i, 0)),
        scratch_shapes=[pltpu.VMEM((N_KV_WIN, GQA_GROUP * WIN, WIN_KEYS), F32),
                        pltpu.VMEM((N_KV_WIN, GQA_GROUP * WIN, WIN_KEYS), BF16),
                        pltpu.VMEM((WIN, D_WIN), F32)],
    )
    return pl.pallas_call(
        functools.partial(_win_kernel, slopes=slopes, seq_len=t),
        grid_spec=grid_spec,
        out_shape=jax.ShapeDtypeStruct((b, t, D_WIN), BF16),
        compiler_params=pltpu.CompilerParams(
            dimension_semantics=("arbitrary", "arbitrary"), vmem_limit_bytes=VMEM_LIMIT),
        name="win_attn",
    )(sink, h, h2, h2, gain)


def _out_proj_kernel(x_ref, oa_ref, ob_ref, w_ref, gain_ref, bias_ref, o_ref):
    for r in range(OUT_TM // PROJ_TM):
        rows = slice(r * PROJ_TM, (r + 1) * PROJ_TM)
        mix = jnp.dot(oa_ref[rows, :], w_ref[0:D_NA, :], preferred_element_type=F32)
        mix = mix + jnp.dot(ob_ref[rows, :], w_ref[D_NA:, :], preferred_element_type=F32)
        y = ALPHA * x_ref[rows, :] + mix
        o_ref[rows, :] = _layer_norm(y, gain_ref[...], bias_ref[...])


def _out_proj_ln(x, oa, ob, w_out, gain, bias, layer):
    t = x.shape[0]
    assert t % OUT_TM == 0 and OUT_TM % PROJ_TM == 0
    return pl.pallas_call(
        _out_proj_kernel,
        grid=(t // OUT_TM,),
        in_specs=[
            pl.BlockSpec((OUT_TM, D_MODEL), lambda i: (i, 0)),
            pl.BlockSpec((OUT_TM, D_NA), lambda i: (i, 0)),
            pl.BlockSpec((OUT_TM, D_WIN), lambda i: (i, 0)),
            pl.BlockSpec((None, D_NA + D_WIN, D_MODEL), lambda i: (layer, 0, 0), pipeline_mode=pl.Buffered(1)),
            pl.BlockSpec((1, D_MODEL), lambda i: (0, 0)),
            pl.BlockSpec((1, D_MODEL), lambda i: (0, 0)),
        ],
        out_specs=pl.BlockSpec((OUT_TM, D_MODEL), lambda i: (i, 0)),
        out_shape=jax.ShapeDtypeStruct((t, D_MODEL), F32),
        compiler_params=pltpu.CompilerParams(
            dimension_semantics=("arbitrary",), vmem_limit_bytes=VMEM_LIMIT),
        name="out_proj_ln",
    )(x, oa, ob, w_out, gain, bias)


def _na_bias_table(rpb):
    qc = np.arange(GRID_W)[:, None]
    kc = np.arange(GRID_W)[None, :]
    dc = np.clip(kc - qc, -(NA_COLS - 1), NA_COLS - 1) + (NA_COLS - 1)
    e = rpb.astype(F32)[:, :, dc]
    e2 = jnp.concatenate([e[:, :-1], e[:, 1:]], axis=-1)
    n_dr = e2.shape[1]
    e2 = e2.reshape(N_SLABS_NA, 2, n_dr, GRID_W, LANES).transpose(0, 2, 1, 3, 4)
    return e2.reshape(N_SLABS_NA, n_dr, 2 * GRID_W, LANES)


def _na_valid_mask():
    qc = np.arange(GRID_W)[:, None]
    kc = np.arange(GRID_W)[None, :]
    start = np.clip(qc - NA_COLS // 2, 0, GRID_W - NA_COLS)
    valid = (kc >= start) & (kc < start + NA_COLS)
    return np.tile(valid, (2, NA_ROWS)).astype(np.float32)


def _trunk(x, params):
    b, t, _ = x.shape
    xf = x.reshape(b * t, D_MODEL)
    mask = jnp.asarray(_na_valid_mask())
    for l in range(DEPTH):
        ln_g = params["ln_gain"][l]
        ln_b = params["ln_bias"][l]
        xf = _ffn_ln(xf, params["ffn1_gate"], params["ffn1_up"], params["ffn1_down"],
                     ln_g[0:1], ln_b[0:1], l)
        h = _in_proj(xf, params["w_in"], l).reshape(b, t, D_IN)
        oa = _na_attention(h, params["na_bias"][l], mask, params["g_na"][l][None, :])
        ob = _win_attention(h, params["win_sink"][l], params["g_win"][l][None, :])
        xf = _out_proj_ln(xf, oa.reshape(b * t, D_NA), ob.reshape(b * t, D_WIN), params["w_out"],
                          ln_g[1:2], ln_b[1:2], l)
        xf = _ffn_ln(xf, params["ffn2_gate"], params["ffn2_up"], params["ffn2_down"],
                     ln_g[2:3], ln_b[2:3], l)
    return xf.reshape(b, t, D_MODEL)


def kernel(x_prompt, x_sample, w_in, w_out, na_rpb, win_sink, g_na, g_win, ffn1_gate, ffn1_up, ffn1_down,
           ffn2_gate, ffn2_up, ffn2_down, ln_gain, ln_bias):
    params = {
        "w_in": w_in.astype(BF16), "w_out": w_out.astype(BF16),
        "ffn1_gate": ffn1_gate.astype(BF16), "ffn1_up": ffn1_up.astype(BF16), "ffn1_down": ffn1_down.astype(BF16),
        "ffn2_gate": ffn2_gate.astype(BF16), "ffn2_up": ffn2_up.astype(BF16), "ffn2_down": ffn2_down.astype(BF16),
        "na_bias": jnp.stack([_na_bias_table(na_rpb[l]) for l in range(DEPTH)]),
        "win_sink": win_sink.astype(F32), "g_na": g_na.astype(F32), "g_win": g_win.astype(F32),
        "ln_gain": ln_gain.astype(F32), "ln_bias": ln_bias.astype(F32),
    }
    return (_trunk(x_prompt, params), _trunk(x_sample, params))
```
